```python
import jax, jax.numpy as jnp
from jax import lax
import numpy as np

D_MODEL = 1024
BATCH = 8
SEQ = 4096
DEPTH = 4
DEC_BATCH = 8
DEC_SEQ = 64
PAST_LEN = 2048

CHUNK = 64
D_MIX = D_MODEL
D_CONV = D_MIX // 2
D_POOL = D_MIX - D_CONV
CONV_WIDTH = 31
POOL_WINDOWS = (2, 4, 8, 16)
N_POOL_GROUPS = len(POOL_WINDOWS)
POOL_GROUP = D_POOL // N_POOL_GROUPS
MAX_POOL = max(POOL_WINDOWS)
D_FF = 4 * D_MODEL
PLE_DIM = 256
EPS = 1e-6

kernel_name = "hybrid_conv_pool_stream_step"


def _rmsnorm(x, g):
    xf = x.astype(jnp.float32)
    r = xf * lax.rsqrt(jnp.mean(xf * xf, axis=-1, keepdims=True) + EPS)
    return (r * g.astype(jnp.float32)).astype(x.dtype)


def _layernorm(x, g, b):
    xf = x.astype(jnp.float32)
    mu = jnp.mean(xf, axis=-1, keepdims=True)
    var = jnp.mean(jnp.square(xf - mu), axis=-1, keepdims=True)
    y = (xf - mu) * lax.rsqrt(var + EPS) * g.astype(jnp.float32) + b.astype(jnp.float32)
    return y.astype(x.dtype)


def _layer(x, p, conv_buf, pool_buf, pos0, w_in, conv_w, conv_b, ln_g, ln_b, pool_w, pool_scale,
           w_out, g_mix, g_ffn, g_ple, w_ff1, w_ff2, w_ple, w_gate):
    B, L, _ = x.shape
    h = _rmsnorm(x, g_mix)
    z = h @ w_in
    a = z[..., :D_CONV]
    gt = z[..., D_CONV:2 * D_CONV]
    u = z[..., 2 * D_CONV:]

    c = a * jax.nn.sigmoid(gt)
    c_pad = jnp.concatenate([conv_buf.astype(c.dtype), c], axis=1)
    cv = lax.conv_general_dilated(
        c_pad, conv_w[:, None, :].astype(c.dtype), (1,), 'VALID',
        dimension_numbers=('NWC', 'WIO', 'NWC'), feature_group_count=D_CONV)
    cv = cv + conv_b
    cv = _layernorm(cv, ln_g, ln_b)
    cv = cv * jax.nn.sigmoid(cv)
    new_conv = c_pad[:, -(CONV_WIDTH - 1):]

    u_pad = jnp.concatenate([pool_buf.astype(u.dtype), u], axis=1)
    new_pool = u_pad[:, -(MAX_POOL - 1):]
    uf = u_pad.astype(jnp.float32)
    csum = jnp.concatenate([jnp.zeros((B, 1, D_POOL), jnp.float32), jnp.cumsum(uf, axis=1)], axis=1)
    csum = csum.reshape(B, L + MAX_POOL, N_POOL_GROUPS, POOL_GROUP)
    pos = pos0 + jnp.arange(L)
    means = []
    for g, w in enumerate(POOL_WINDOWS):
        s = csum[:, MAX_POOL:, g] - csum[:, MAX_POOL - w:MAX_POOL - w + L, g]
        cnt = jnp.minimum(w, pos + 1).astype(jnp.float32)
        means.append(s / cnt[None, :, None])
    pooled = jnp.stack(means, axis=2)
    d = pooled - uf[:, MAX_POOL - 1:].reshape(B, L, N_POOL_GROUPS, POOL_GROUP)
    pm = jnp.einsum('blgc,gcd->blgd', d.astype(x.dtype), pool_w).reshape(B, L, D_POOL) * pool_scale

    x = x + jnp.concatenate([cv.astype(x.dtype), pm.astype(x.dtype)], axis=-1) @ w_out

    h2 = _rmsnorm(x, g_ffn)
    x = x + jnp.square(jax.nn.relu(h2 @ w_ff1)) @ w_ff2

    gate = jax.nn.sigmoid(_rmsnorm(x, g_ple) @ w_gate)
    x = x + gate * (p @ w_ple)
    return x, new_conv, new_pool


def _trunk(x, p, conv_bufs, pool_bufs, pos0, w_in, conv_w, conv_b, ln_g, ln_b, pool_w, pool_scale,
           w_out, g_mix, g_ffn, g_ple, w_ff1, w_ff2, w_ple, w_gate, g_final):
    new_convs, new_pools = [], []
    for i in range(DEPTH):
        x, nc, npl = _layer(x, p[i], conv_bufs[i], pool_bufs[i], pos0, w_in[i], conv_w[i], conv_b[i],
                            ln_g[i], ln_b[i], pool_w[i], pool_scale[i], w_out[i], g_mix[i], g_ffn[i],
                            g_ple[i], w_ff1[i], w_ff2[i], w_ple[i], w_gate[i])
        new_convs.append(nc)
        new_pools.append(npl)
    return _rmsnorm(x, g_final), jnp.stack(new_convs, 0), jnp.stack(new_pools, 0)


def setup_inputs(seed: int = 0) -> dict:
    key = jax.random.key(seed)
    ks = jax.random.split(key, 24)
    f32 = jnp.float32
    nrm = lambda k, s, sc: (jax.random.normal(k, s, f32) * sc).astype(f32)
    return {
        "x_prompt": nrm(ks[0], (BATCH, SEQ, D_MODEL), 1.0),
        "x_sample": nrm(ks[1], (DEC_BATCH, DEC_SEQ, D_MODEL), 1.0),
        "p_prompt": nrm(ks[2], (DEPTH, BATCH, SEQ, PLE_DIM), 1.0),
        "p_sample": nrm(ks[3], (DEPTH, DEC_BATCH, DEC_SEQ, PLE_DIM), 1.0),
        "cache_conv": nrm(ks[4], (DEPTH, DEC_BATCH, CONV_WIDTH - 1, D_CONV), 1.0),
        "cache_pool": nrm(ks[5], (DEPTH, DEC_BATCH, MAX_POOL - 1, D_POOL), 1.0),
        "w_in": nrm(ks[6], (DEPTH, D_MODEL, 2 * D_CONV + D_POOL), D_MODEL ** -0.5),
        "conv_w": nrm(ks[7], (DEPTH, CONV_WIDTH, D_CONV), CONV_WIDTH ** -0.5),
        "conv_b": nrm(ks[8], (DEPTH, D_CONV), 0.02),
        "ln_g": 1.0 + nrm(ks[9], (DEPTH, D_CONV), 0.05),
        "ln_b": nrm(ks[10], (DEPTH, D_CONV), 0.02),
        "pool_w": nrm(ks[11], (DEPTH, N_POOL_GROUPS, POOL_GROUP, POOL_GROUP), POOL_GROUP ** -0.5),
        "pool_scale": 1.0 + nrm(ks[12], (DEPTH, D_POOL), 0.05),
        "w_out": nrm(ks[13], (DEPTH, D_MIX, D_MODEL), D_MIX ** -0.5),
        "g_mix": 1.0 + nrm(ks[14], (DEPTH, D_MODEL), 0.05),
        "g_ffn": 1.0 + nrm(ks[15], (DEPTH, D_MODEL), 0.05),
        "g_ple": 1.0 + nrm(ks[16], (DEPTH, D_MODEL), 0.05),
        "w_ff1": nrm(ks[17], (DEPTH, D_MODEL, D_FF), D_MODEL ** -0.5),
        "w_ff2": nrm(ks[18], (DEPTH, D_FF, D_MODEL), D_FF ** -0.5),
        "w_ple": nrm(ks[19], (DEPTH, PLE_DIM, D_MODEL), PLE_DIM ** -0.5),
        "w_gate": nrm(ks[20], (DEPTH, D_MODEL, D_MODEL), D_MODEL ** -0.5),
        "g_final": 1.0 + nrm(ks[21], (D_MODEL,), 0.05),
    }


def reference(x_prompt, x_sample, p_prompt, p_sample, cache_conv, cache_pool, w_in, conv_w, conv_b,
              ln_g, ln_b, pool_w, pool_scale, w_out, g_mix, g_ffn, g_ple, w_ff1, w_ff2, w_ple, w_gate,
              g_final):
    B = x_prompt.shape[0]
    zero_conv = jnp.zeros((DEPTH, B, CONV_WIDTH - 1, D_CONV), x_prompt.dtype)
    zero_pool = jnp.zeros((DEPTH, B, MAX_POOL - 1, D_POOL), x_prompt.dtype)
    y_prompt, new_conv_prompt, new_pool_prompt = _trunk(
        x_prompt, p_prompt, zero_conv, zero_pool, 0, w_in, conv_w, conv_b, ln_g, ln_b, pool_w,
        pool_scale, w_out, g_mix, g_ffn, g_ple, w_ff1, w_ff2, w_ple, w_gate, g_final)
    y_sample, new_conv_sample, new_pool_sample = _trunk(
        x_sample, p_sample, cache_conv, cache_pool, PAST_LEN, w_in, conv_w, conv_b, ln_g, ln_b, pool_w,
        pool_scale, w_out, g_mix, g_ffn, g_ple, w_ff1, w_ff2, w_ple, w_gate, g_final)
    return (y_prompt, y_sample, new_conv_prompt, new_pool_prompt, new_conv_sample, new_pool_sample)
```

```python
import functools

import jax
import jax.numpy as jnp
from jax import lax
from jax.experimental import pallas as pl
from jax.experimental.pallas import tpu as pltpu

D_MODEL = 1024
D_CONV = 512
D_POOL = 512
CONV_WIDTH = 31
CONV_HALO = CONV_WIDTH - 1
POOL_WINDOWS = (2, 4, 8, 16)
POOL_GROUP = D_POOL // len(POOL_WINDOWS)
POOL_HALO = max(POOL_WINDOWS) - 1
D_FF = 4 * D_MODEL
PLE_DIM = 256
PAST_LEN = 2048
EPS = 1e-6

SUBLANES = 8
LANES = 128
CONV_PAD = 32
POOL_PAD = 16
N_COLS = D_CONV // LANES
ROW_GROUP = 32
FF_CHUNK = 1024
PROMPT_TILE = 512
VMEM_LIMIT_BYTES = 60000 * 1024

assert POOL_GROUP == LANES and CONV_PAD >= CONV_HALO and POOL_PAD >= POOL_HALO


def _rms_scale(x, gain_row):
    ms = jnp.mean(x * x, axis=-1, keepdims=True)
    return x * lax.rsqrt(ms + EPS) * gain_row


def _col(j):
    return slice(j * LANES, (j + 1) * LANES)


def _layer_kernel(x_ref, p_ref, cinit_ref, uinit_ref,
                  w_in_ref, conv_w_ref, conv_b_ref, ln_g_ref, ln_b_ref, pool_w_ref, pool_scale_ref,
                  w_out_ref, g_mix_ref, g_ffn_ref, g_ple_ref, w_ff1_ref, w_ff2_ref, w_ple_ref,
                  w_gate_ref, g_final_ref,
                  y_ref, newc_ref, newu_ref,
                  cpad_ref, upad_ref, wtap_ref, mix_ref, dpool_ref,
                  *, tile, pos0, apply_final):
    f32, bf16 = jnp.float32, jnp.bfloat16
    b = pl.program_id(0)
    t = pl.program_id(1)
    last_t = pl.num_programs(1) - 1

    @pl.when(jnp.logical_and(b == 0, t == 0))
    def _():
        for k in range(CONV_WIDTH):
            wtap_ref[k] = jnp.broadcast_to(conv_w_ref[pl.ds(k, 1), :], (SUBLANES, D_CONV))

    @pl.when(t == 0)
    def _():
        for j in range(N_COLS):
            cpad_ref[j, 0:SUBLANES, :] = jnp.zeros((SUBLANES, LANES), f32)
            cpad_ref[j, CONV_PAD - CONV_HALO:CONV_PAD, :] = cinit_ref[:, _col(j)]
            upad_ref[j, 0:SUBLANES, :] = jnp.zeros((SUBLANES, LANES), f32)
            upad_ref[j, POOL_PAD - POOL_HALO:POOL_PAD, :] = uinit_ref[:, _col(j)]

    @pl.when(t > 0)
    def _():
        for j in range(N_COLS):
            cpad_ref[j, 0:CONV_PAD, :] = cpad_ref[j, tile:tile + CONV_PAD, :]
            upad_ref[j, 0:POOL_PAD, :] = upad_ref[j, tile:tile + POOL_PAD, :]

    x = x_ref[...]
    h = _rms_scale(x, g_mix_ref[...]).astype(bf16)
    z = jnp.dot(h, w_in_ref[...], preferred_element_type=f32)
    c = z[:, :D_CONV] * jax.nn.sigmoid(z[:, D_CONV:2 * D_CONV])
    u = z[:, 2 * D_CONV:]
    for j in range(N_COLS):
        cpad_ref[j, CONV_PAD:CONV_PAD + tile, :] = c[:, _col(j)]
        upad_ref[j, POOL_PAD:POOL_PAD + tile, :] = u[:, _col(j)]

    @pl.when(t == last_t)
    def _():
        for j in range(N_COLS):
            newc_ref[:, _col(j)] = cpad_ref[j, tile + CONV_PAD - CONV_HALO:tile + CONV_PAD, :]
            newu_ref[:, _col(j)] = upad_ref[j, tile + POOL_PAD - POOL_HALO:tile + POOL_PAD, :]

    conv_b = conv_b_ref[...]
    ln_g = ln_g_ref[...]
    ln_b = ln_b_ref[...]
    reps = ROW_GROUP // SUBLANES
    for r0 in range(0, tile, ROW_GROUP):
        cols = []
        for j in range(N_COLS):
            acc = jnp.broadcast_to(conv_b[:, _col(j)], (ROW_GROUP, LANES))
            for k in range(CONV_WIDTH):
                wk = wtap_ref[k, :, _col(j)]
                seg = cpad_ref[j, pl.ds(r0 + CONV_PAD - CONV_HALO + k, ROW_GROUP), :]
                acc = acc + seg * jnp.concatenate([wk] * reps, axis=0)
            cols.append(acc)
        cv = jnp.concatenate(cols, axis=-1)
        mu = jnp.mean(cv, axis=-1, keepdims=True)
        var = jnp.mean(jnp.square(cv - mu), axis=-1, keepdims=True)
        cv = (cv - mu) * lax.rsqrt(var + EPS) * ln_g + ln_b
        cv = cv * jax.nn.sigmoid(cv)
        mix_ref[r0:r0 + ROW_GROUP, 0:D_CONV] = cv.astype(bf16)

        for g, w in enumerate(POOL_WINDOWS):
            cur = upad_ref[g, pl.ds(r0 + POOL_PAD, ROW_GROUP), :]
            s = cur
            for i in range(1, w):
                s = s + upad_ref[g, pl.ds(r0 + POOL_PAD - i, ROW_GROUP), :]
            if r0 >= POOL_HALO:
                mean = s * (1.0 / w)
            else:
                pos = pos0 + t * tile + r0 + lax.broadcasted_iota(jnp.int32, (ROW_GROUP, LANES), 0)
                mean = s / jnp.minimum(w, pos + 1).astype(f32)
            dpool_ref[r0:r0 + ROW_GROUP, _col(g)] = (mean - cur).astype(bf16)

    for g in range(len(POOL_WINDOWS)):
        pm = jnp.dot(dpool_ref[:, _col(g)], pool_w_ref[g], preferred_element_type=f32)
        pm = pm * pool_scale_ref[:, _col(g)]
        mix_ref[:, D_CONV + g * LANES:D_CONV + (g + 1) * LANES] = pm.astype(bf16)

    x = x + jnp.dot(mix_ref[...], w_out_ref[...], preferred_element_type=f32)

    h = _rms_scale(x, g_ffn_ref[...]).astype(bf16)
    for f0 in range(0, D_FF, FF_CHUNK):
        a = jnp.dot(h, w_ff1_ref[:, f0:f0 + FF_CHUNK], preferred_element_type=f32)
        a = jnp.square(jnp.maximum(a, 0.0)).astype(bf16)
        x = x + jnp.dot(a, w_ff2_ref[f0:f0 + FF_CHUNK, :], preferred_element_type=f32)

    h = _rms_scale(x, g_ple_ref[...]).astype(bf16)
    gate = jax.nn.sigmoid(jnp.dot(h, w_gate_ref[...], preferred_element_type=f32))
    pe = jnp.dot(p_ref[...].astype(bf16), w_ple_ref[...], preferred_element_type=f32)
    x = x + gate * pe
    if apply_final:
        x = _rms_scale(x, g_final_ref[...])
    y_ref[...] = x


def _layer_call(layer, x, p, cinit, uinit, weights, g_final, *, tile, pos0, apply_final):
    batch, seq, _ = x.shape
    assert seq % tile == 0 and tile % ROW_GROUP == 0 and tile >= CONV_PAD
    grid = (batch, seq // tile)

    def resident(shape):
        zeros = (0,) * len(shape)
        return pl.BlockSpec((None,) + shape, lambda b, t: (layer,) + zeros,
                            pipeline_mode=pl.Buffered(1))

    in_specs = [
        pl.BlockSpec((None, tile, D_MODEL), lambda b, t: (b, t, 0)),
        pl.BlockSpec((None, None, tile, PLE_DIM), lambda b, t: (layer, b, t, 0)),
        pl.BlockSpec((None, None, CONV_HALO, D_CONV), lambda b, t: (layer, b, 0, 0)),
        pl.BlockSpec((None, None, POOL_HALO, D_POOL), lambda b, t: (layer, b, 0, 0)),
    ] + [resident(w.shape[1:]) for w in weights] + [
        pl.BlockSpec((1, D_MODEL), lambda b, t: (0, 0), pipeline_mode=pl.Buffered(1)),
    ]
    out_specs = [
        pl.BlockSpec((None, tile, D_MODEL), lambda b, t: (b, t, 0)),
        pl.BlockSpec((None, CONV_HALO, D_CONV), lambda b, t: (b, 0, 0)),
        pl.BlockSpec((None, POOL_HALO, D_POOL), lambda b, t: (b, 0, 0)),
    ]
    out_shape = [
        jax.ShapeDtypeStruct((batch, seq, D_MODEL), jnp.float32),
        jax.ShapeDtypeStruct((batch, CONV_HALO, D_CONV), jnp.float32),
        jax.ShapeDtypeStruct((batch, POOL_HALO, D_POOL), jnp.float32),
    ]
    scratch_shapes = [
        pltpu.VMEM((N_COLS, CONV_PAD + tile, LANES), jnp.float32),
        pltpu.VMEM((N_COLS, POOL_PAD + tile, LANES), jnp.float32),
        pltpu.VMEM((CONV_WIDTH, SUBLANES, D_CONV), jnp.float32),
        pltpu.VMEM((tile, D_MODEL), jnp.bfloat16),
        pltpu.VMEM((tile, D_POOL), jnp.bfloat16),
    ]
    return pl.pallas_call(
        functools.partial(_layer_kernel, tile=tile, pos0=pos0, apply_final=apply_final),
        grid=grid,
        in_specs=in_specs,
        out_specs=out_specs,
        out_shape=out_shape,
        scratch_shapes=scratch_shapes,
        compiler_params=pltpu.CompilerParams(
            dimension_semantics=("arbitrary", "arbitrary"),
            vmem_limit_bytes=VMEM_LIMIT_BYTES),
        name=f"layer{layer}_tile{tile}",
    )(x, p, cinit, uinit, *weights, g_final)


def _trunk(x, p, cinit, uinit, weights, g_final, *, tile, pos0):
    depth = p.shape[0]
    new_c, new_u = [], []
    for layer in range(depth):
        x, nc, nu = _layer_call(layer, x, p, cinit, uinit, weights, g_final, tile=tile, pos0=pos0,
                                apply_final=(layer == depth - 1))
        new_c.append(nc)
        new_u.append(nu)
    return x, jnp.stack(new_c, 0), jnp.stack(new_u, 0)


def kernel(x_prompt, x_sample, p_prompt, p_sample, cache_conv, cache_pool, w_in, conv_w, conv_b, ln_g, ln_b, pool_w, pool_scale, w_out, g_mix, g_ffn, g_ple, w_ff1, w_ff2, w_ple, w_gate, g_final):
    bf16 = jnp.bfloat16
    row = lambda v: v[:, None, :]
    weights = (w_in.astype(bf16), conv_w, row(conv_b), row(ln_g), row(ln_b), pool_w.astype(bf16),
               row(pool_scale), w_out.astype(bf16), row(g_mix), row(g_ffn), row(g_ple),
               w_ff1.astype(bf16), w_ff2.astype(bf16), w_ple.astype(bf16), w_gate.astype(bf16))
    g_fin = g_final[None, :]
    depth, batch = p_prompt.shape[0], x_prompt.shape[0]
    zero_conv = jnp.zeros((depth, batch, CONV_HALO, D_CONV), x_prompt.dtype)
    zero_pool = jnp.zeros((depth, batch, POOL_HALO, D_POOL), x_prompt.dtype)
    y_p, nc_p, nu_p = _trunk(x_prompt, p_prompt, zero_conv, zero_pool, weights, g_fin,
                             tile=PROMPT_TILE, pos0=0)
    y_s, nc_s, nu_s = _trunk(x_sample, p_sample, cache_conv, cache_pool, weights, g_fin,
                             tile=x_sample.shape[1], pos0=PAST_LEN)
    return (y_p, y_s, nc_p, nu_p, nc_s, nu_s)
```

```python
import functools

import jax
import jax.numpy as jnp
from jax import lax
from jax.experimental import pallas as pl
from jax.experimental.pallas import tpu as pltpu

D_MODEL = 1024
D_CONV = 512
D_POOL = 512
CONV_WIDTH = 31
CONV_HALO = CONV_WIDTH - 1
POOL_WINDOWS = (2, 4, 8, 16)
POOL_GROUP = D_POOL // len(POOL_WINDOWS)
POOL_HALO = max(POOL_WINDOWS) - 1
D_FF = 4 * D_MODEL
PLE_DIM = 256
PAST_LEN = 2048
EPS = 1e-6

SUBLANES = 8
LANES = 128
CONV_PAD = 32
POOL_PAD = 16
N_COLS = D_CONV // LANES
ROW_GROUP = 32
FF_CHUNK = 1024
TILE_ROWS = 512
VMEM_LIMIT_BYTES = 60000 * 1024

assert POOL_GROUP == LANES and CONV_PAD >= CONV_HALO and POOL_PAD >= POOL_HALO


def _rms_scale(x, gain_row):
    ms = jnp.mean(x * x, axis=-1, keepdims=True)
    return x * lax.rsqrt(ms + EPS) * gain_row


def _col(j):
    return slice(j * LANES, (j + 1) * LANES)


def _load_history(cinit_ref, uinit_ref, cpad_ref, upad_ref, *, t, nseg, seg_len):
    @pl.when(t == 0)
    def _():
        for s in range(nseg):
            for j in range(N_COLS):
                cpad_ref[j, s, 0:SUBLANES, :] = jnp.zeros((SUBLANES, LANES), jnp.float32)
                cpad_ref[j, s, CONV_PAD - CONV_HALO:CONV_PAD, :] = cinit_ref[s, :, _col(j)]
                upad_ref[j, s, 0:SUBLANES, :] = jnp.zeros((SUBLANES, LANES), jnp.float32)
                upad_ref[j, s, POOL_PAD - POOL_HALO:POOL_PAD, :] = uinit_ref[s, :, _col(j)]

    @pl.when(t > 0)
    def _():
        for s in range(nseg):
            for j in range(N_COLS):
                cpad_ref[j, s, 0:CONV_PAD, :] = cpad_ref[j, s, seg_len:seg_len + CONV_PAD, :]
                upad_ref[j, s, 0:POOL_PAD, :] = upad_ref[j, s, seg_len:seg_len + POOL_PAD, :]


def _store_history(newc_ref, newu_ref, cpad_ref, upad_ref, *, t, last_t, nseg, seg_len):
    @pl.when(t == last_t)
    def _():
        for s in range(nseg):
            for j in range(N_COLS):
                newc_ref[s, :, _col(j)] = cpad_ref[j, s, seg_len + CONV_PAD - CONV_HALO:seg_len + CONV_PAD, :]
                newu_ref[s, :, _col(j)] = upad_ref[j, s, seg_len + POOL_PAD - POOL_HALO:seg_len + POOL_PAD, :]


def _mix_in(x_ref, w_in_ref, g_mix_ref, cpad_ref, upad_ref, *, nseg, seg_len):
    f32, bf16 = jnp.float32, jnp.bfloat16
    rows = nseg * seg_len
    h = _rms_scale(x_ref[...].reshape(rows, D_MODEL), g_mix_ref[...]).astype(bf16)
    z = jnp.dot(h, w_in_ref[...], preferred_element_type=f32)
    c = z[:, :D_CONV] * jax.nn.sigmoid(z[:, D_CONV:2 * D_CONV])
    u = z[:, 2 * D_CONV:]
    for s in range(nseg):
        for j in range(N_COLS):
            cpad_ref[j, s, CONV_PAD:CONV_PAD + seg_len, :] = c[s * seg_len:(s + 1) * seg_len, _col(j)]
            upad_ref[j, s, POOL_PAD:POOL_PAD + seg_len, :] = u[s * seg_len:(s + 1) * seg_len, _col(j)]


def _mix_rows(s, r0, conv_b_ref, ln_g_ref, ln_b_ref, cpad_ref, upad_ref, wtap_ref, mix_ref, dpool_ref,
              *, t, seg_len, pos0):
    f32, bf16 = jnp.float32, jnp.bfloat16
    m0 = s * seg_len + r0
    conv_b = conv_b_ref[...]
    reps = ROW_GROUP // SUBLANES
    cols = []
    for j in range(N_COLS):
        acc = jnp.broadcast_to(conv_b[:, _col(j)], (ROW_GROUP, LANES))
        for k in range(CONV_WIDTH):
            wk = wtap_ref[k, :, _col(j)]
            seg = cpad_ref[j, s, pl.ds(r0 + CONV_PAD - CONV_HALO + k, ROW_GROUP), :]
            acc = acc + seg * jnp.concatenate([wk] * reps, axis=0)
        cols.append(acc)
    cv = jnp.concatenate(cols, axis=-1)
    mu = jnp.mean(cv, axis=-1, keepdims=True)
    var = jnp.mean(jnp.square(cv - mu), axis=-1, keepdims=True)
    cv = (cv - mu) * lax.rsqrt(var + EPS) * ln_g_ref[...] + ln_b_ref[...]
    cv = cv * jax.nn.sigmoid(cv)
    mix_ref[m0:m0 + ROW_GROUP, 0:D_CONV] = cv.astype(bf16)

    for g, w in enumerate(POOL_WINDOWS):
        cur = upad_ref[g, s, pl.ds(r0 + POOL_PAD, ROW_GROUP), :]
        total = cur
        for i in range(1, w):
            total = total + upad_ref[g, s, pl.ds(r0 + POOL_PAD - i, ROW_GROUP), :]
        if r0 >= POOL_HALO:
            mean = total * (1.0 / w)
        else:
            pos = pos0 + t * seg_len + r0 + lax.broadcasted_iota(jnp.int32, (ROW_GROUP, LANES), 0)
            mean = total / jnp.minimum(w, pos + 1).astype(f32)
        dpool_ref[m0:m0 + ROW_GROUP, _col(g)] = (mean - cur).astype(bf16)


def _mix_out(x_ref, pool_w_ref, pool_scale_ref, w_out_ref, mix_ref, dpool_ref, *, rows):
    f32, bf16 = jnp.float32, jnp.bfloat16
    for g in range(len(POOL_WINDOWS)):
        pm = jnp.dot(dpool_ref[:, _col(g)], pool_w_ref[g], preferred_element_type=f32)
        pm = pm * pool_scale_ref[:, _col(g)]
        mix_ref[:, D_CONV + g * LANES:D_CONV + (g + 1) * LANES] = pm.astype(bf16)
    return (x_ref[...].reshape(rows, D_MODEL)
            + jnp.dot(mix_ref[...], w_out_ref[...], preferred_element_type=f32))


def _mlp_chunk(i, x, h, w_ff1_ref, w_ff2_ref):
    f0 = i * FF_CHUNK
    a = jnp.dot(h, w_ff1_ref[:, f0:f0 + FF_CHUNK], preferred_element_type=jnp.float32)
    a = jnp.square(jnp.maximum(a, 0.0)).astype(jnp.bfloat16)
    return x + jnp.dot(a, w_ff2_ref[f0:f0 + FF_CHUNK, :], preferred_element_type=jnp.float32)


def _ple_stage(x, p, g_ple_ref, w_ple_ref, w_gate_ref, g_final_ref, *, apply_final):
    f32, bf16 = jnp.float32, jnp.bfloat16
    h = _rms_scale(x, g_ple_ref[...]).astype(bf16)
    gate = jax.nn.sigmoid(jnp.dot(h, w_gate_ref[...], preferred_element_type=f32))
    pe = jnp.dot(p.astype(bf16), w_ple_ref[...], preferred_element_type=f32)
    x = x + gate * pe
    if apply_final:
        x = _rms_scale(x, g_final_ref[...])
    return x


def _layer_kernel(x_ref, p_ref, cinit_ref, uinit_ref,
                  w_in_ref, conv_w_ref, conv_b_ref, ln_g_ref, ln_b_ref, pool_w_ref, pool_scale_ref,
                  w_out_ref, g_mix_ref, g_ffn_ref, g_ple_ref, w_ff1_ref, w_ff2_ref, w_ple_ref,
                  w_gate_ref, g_final_ref,
                  y_ref, newc_ref, newu_ref,
                  cpad_ref, upad_ref, wtap_ref, mix_ref, dpool_ref, *lag_refs,
                  nseg, seg_len, tiles_per_seq, pos0, apply_final, lag):
    rows = nseg * seg_len
    n_chunks = D_FF // FF_CHUNK
    g = pl.program_id(0)
    ga = jnp.minimum(g, pl.num_programs(0) - 1 - lag)
    t = lax.rem(ga, tiles_per_seq)

    @pl.when(g == 0)
    def _():
        for k in range(CONV_WIDTH):
            wtap_ref[k] = jnp.broadcast_to(conv_w_ref[pl.ds(k, 1), :], (SUBLANES, D_CONV))
        if lag:
            lag_refs[0][...] = jnp.zeros((rows, D_MODEL), jnp.float32)

    ple = functools.partial(_ple_stage, g_ple_ref=g_ple_ref, w_ple_ref=w_ple_ref, w_gate_ref=w_gate_ref,
                            g_final_ref=g_final_ref, apply_final=apply_final)
    row_groups = [
        functools.partial(_mix_rows, s, r0, conv_b_ref, ln_g_ref, ln_b_ref, cpad_ref, upad_ref, wtap_ref,
                          mix_ref, dpool_ref, t=t, seg_len=seg_len, pos0=pos0)
        for s in range(nseg) for r0 in range(0, seg_len, ROW_GROUP)]
    mix_in = functools.partial(_mix_in, x_ref, w_in_ref, g_mix_ref, cpad_ref, upad_ref,
                               nseg=nseg, seg_len=seg_len)
    mix_out = functools.partial(_mix_out, x_ref, pool_w_ref, pool_scale_ref, w_out_ref, mix_ref,
                                dpool_ref, rows=rows)

    _load_history(cinit_ref, uinit_ref, cpad_ref, upad_ref, t=t, nseg=nseg, seg_len=seg_len)
    p = p_ref[...].reshape(rows, PLE_DIM)
    if lag:
        x1_ref, = lag_refs
        mix_in()
        x = x1_ref[...]
        h = _rms_scale(x, g_ffn_ref[...]).astype(jnp.bfloat16)
        for i in range(n_chunks):
            x = _mlp_chunk(i, x, h, w_ff1_ref, w_ff2_ref)
            for piece in row_groups[i::n_chunks]:
                piece()
        y_ref[...] = ple(x, p).reshape(nseg, seg_len, D_MODEL)
        x1_ref[...] = mix_out()
    else:
        mix_in()
        for piece in row_groups:
            piece()
        x = mix_out()
        h = _rms_scale(x, g_ffn_ref[...]).astype(jnp.bfloat16)
        for i in range(n_chunks):
            x = _mlp_chunk(i, x, h, w_ff1_ref, w_ff2_ref)
        y_ref[...] = ple(x, p).reshape(nseg, seg_len, D_MODEL)
    _store_history(newc_ref, newu_ref, cpad_ref, upad_ref, t=t, last_t=tiles_per_seq - 1,
                   nseg=nseg, seg_len=seg_len)


def _layer_call(layer, x, p, cinit, uinit, weights, g_final, *, nseg, seg_len, pos0, apply_final, lag):
    batch, seq, _ = x.shape
    assert batch % nseg == 0 and seq % seg_len == 0
    assert seg_len % ROW_GROUP == 0 and seg_len >= CONV_PAD
    tiles_per_seq = seq // seg_len
    n_tiles = (batch // nseg) * tiles_per_seq
    rows = nseg * seg_len

    def cur(g):
        ga = jnp.minimum(g, n_tiles - 1)
        return ga // tiles_per_seq, ga % tiles_per_seq

    def prev(g):
        gb = jnp.maximum(g - lag, 0)
        return gb // tiles_per_seq, gb % tiles_per_seq

    def resident(shape):
        zeros = (0,) * len(shape)
        return pl.BlockSpec((None,) + shape, lambda g: (layer,) + zeros,
                            pipeline_mode=pl.Buffered(1))

    in_specs = [
        pl.BlockSpec((nseg, seg_len, D_MODEL), lambda g: cur(g) + (0,)),
        pl.BlockSpec((None, nseg, seg_len, PLE_DIM), lambda g: (layer,) + prev(g) + (0,)),
        pl.BlockSpec((None, nseg, CONV_HALO, D_CONV), lambda g: (layer, cur(g)[0], 0, 0)),
        pl.BlockSpec((None, nseg, POOL_HALO, D_POOL), lambda g: (layer, cur(g)[0], 0, 0)),
    ] + [resident(w.shape[1:]) for w in weights] + [
        pl.BlockSpec((1, D_MODEL), lambda g: (0, 0), pipeline_mode=pl.Buffered(1)),
    ]
    out_specs = [
        pl.BlockSpec((nseg, seg_len, D_MODEL), lambda g: prev(g) + (0,)),
        pl.BlockSpec((nseg, CONV_HALO, D_CONV), lambda g: (cur(g)[0], 0, 0)),
        pl.BlockSpec((nseg, POOL_HALO, D_POOL), lambda g: (cur(g)[0], 0, 0)),
    ]
    out_shape = [
        jax.ShapeDtypeStruct((batch, seq, D_MODEL), jnp.float32),
        jax.ShapeDtypeStruct((batch, CONV_HALO, D_CONV), jnp.float32),
        jax.ShapeDtypeStruct((batch, POOL_HALO, D_POOL), jnp.float32),
    ]
    scratch_shapes = [
        pltpu.VMEM((N_COLS, nseg, CONV_PAD + seg_len, LANES), jnp.float32),
        pltpu.VMEM((N_COLS, nseg, POOL_PAD + seg_len, LANES), jnp.float32),
        pltpu.VMEM((CONV_WIDTH, SUBLANES, D_CONV), jnp.float32),
        pltpu.VMEM((rows, D_MODEL), jnp.bfloat16),
        pltpu.VMEM((rows, D_POOL), jnp.bfloat16),
    ] + [pltpu.VMEM((rows, D_MODEL), jnp.float32)] * lag
    return pl.pallas_call(
        functools.partial(_layer_kernel, nseg=nseg, seg_len=seg_len, tiles_per_seq=tiles_per_seq,
                          pos0=pos0, apply_final=apply_final, lag=lag),
        grid=(n_tiles + lag,),
        in_specs=in_specs,
        out_specs=out_specs,
        out_shape=out_shape,
        scratch_shapes=scratch_shapes,
        compiler_params=pltpu.CompilerParams(
            dimension_semantics=("arbitrary",),
            vmem_limit_bytes=VMEM_LIMIT_BYTES),
        name=f"layer{layer}_{nseg}x{seg_len}",
    )(x, p, cinit, uinit, *weights, g_final)


def _trunk(x, p, cinit, uinit, weights, g_final, *, pos0):
    depth, batch, seq = p.shape[0], x.shape[0], x.shape[1]
    seg_len = min(seq, TILE_ROWS)
    nseg = min(batch, TILE_ROWS // seg_len)
    lag = 1 if (batch // nseg) * (seq // seg_len) > 1 else 0
    new_c, new_u = [], []
    for layer in range(depth):
        x, nc, nu = _layer_call(layer, x, p, cinit, uinit, weights, g_final, nseg=nseg, seg_len=seg_len,
                                pos0=pos0, apply_final=(layer == depth - 1), lag=lag)
        new_c.append(nc)
        new_u.append(nu)
    return x, jnp.stack(new_c, 0), jnp.stack(new_u, 0)


def kernel(x_prompt, x_sample, p_prompt, p_sample, cache_conv, cache_pool, w_in, conv_w, conv_b, ln_g, ln_b, pool_w, pool_scale, w_out, g_mix, g_ffn, g_ple, w_ff1, w_ff2, w_ple, w_gate, g_final):
    bf16 = jnp.bfloat16
    row = lambda v: v[:, None, :]
    weights = (w_in.astype(bf16), conv_w, row(conv_b), row(ln_g), row(ln_b), pool_w.astype(bf16),
               row(pool_scale), w_out.astype(bf16), row(g_mix), row(g_ffn), row(g_ple),
               w_ff1.astype(bf16), w_ff2.astype(bf16), w_ple.astype(bf16), w_gate.astype(bf16))
    g_fin = g_final[None, :]
    depth, batch = p_prompt.shape[0], x_prompt.shape[0]
    zero_conv = jnp.zeros((depth, batch, CONV_HALO, D_CONV), x_prompt.dtype)
    zero_pool = jnp.zeros((depth, batch, POOL_HALO, D_POOL), x_prompt.dtype)
    y_p, nc_p, nu_p = _trunk(x_prompt, p_prompt, zero_conv, zero_pool, weights, g_fin, pos0=0)
    y_s, nc_s, nu_s = _trunk(x_sample, p_sample, cache_conv, cache_pool, weights, g_fin, pos0=PAST_LEN)
    return (y_p, y_s, nc_p, nu_p, nc_s, nu_s)
```

```python
import functools

import jax
import jax.numpy as jnp
from jax import lax
from jax.experimental import pallas as pl
from jax.experimental.pallas import tpu as pltpu

D_MODEL = 1024
D_CONV = 512
D_POOL = 512
CONV_WIDTH = 31
CONV_HALO = CONV_WIDTH - 1
POOL_WINDOWS = (2, 4, 8, 16)
POOL_GROUP = D_POOL // len(POOL_WINDOWS)
POOL_HALO = max(POOL_WINDOWS) - 1
D_FF = 4 * D_MODEL
PLE_DIM = 256
PAST_LEN = 2048
EPS = 1e-6

SUBLANES = 8
LANES = 128
BF16_ROWS = 16
CONV_PAD = 32
POOL_PAD = 16
N_COLS = D_CONV // LANES
ROW_GROUP = 32
FF_CHUNK = 1024
TILE_ROWS = 512
VMEM_LIMIT_BYTES = 60000 * 1024
N_SMALL = 10
N_BIG = 5

assert POOL_GROUP == LANES and CONV_PAD >= CONV_HALO and POOL_PAD >= POOL_HALO


def _rms_scale(x, gain_row):
    ms = jnp.mean(x * x, axis=-1, keepdims=True)
    return x * lax.rsqrt(ms + EPS) * gain_row


def _col(j):
    return slice(j * LANES, (j + 1) * LANES)


def _load_history(cinit_ref, uinit_ref, cpad_ref, upad_ref, *, t, nseg, seg_len):
    @pl.when(t == 0)
    def _():
        for s in range(nseg):
            for j in range(N_COLS):
                cpad_ref[j, s, 0:SUBLANES, :] = jnp.zeros((SUBLANES, LANES), jnp.float32)
                cpad_ref[j, s, CONV_PAD - CONV_HALO:CONV_PAD, :] = cinit_ref[s, :, _col(j)]
                upad_ref[j, s, 0:SUBLANES, :] = jnp.zeros((SUBLANES, LANES), jnp.float32)
                upad_ref[j, s, POOL_PAD - POOL_HALO:POOL_PAD, :] = uinit_ref[s, :, _col(j)]

    @pl.when(t > 0)
    def _():
        for s in range(nseg):
            for j in range(N_COLS):
                cpad_ref[j, s, 0:CONV_PAD, :] = cpad_ref[j, s, seg_len:seg_len + CONV_PAD, :]
                upad_ref[j, s, 0:POOL_PAD, :] = upad_ref[j, s, seg_len:seg_len + POOL_PAD, :]


def _store_history(newc_ref, newu_ref, cpad_ref, upad_ref, *, t, last_t, nseg, seg_len):
    @pl.when(t == last_t)
    def _():
        for s in range(nseg):
            for j in range(N_COLS):
                newc_ref[s, :, _col(j)] = cpad_ref[j, s, seg_len + CONV_PAD - CONV_HALO:seg_len + CONV_PAD, :]
                newu_ref[s, :, _col(j)] = upad_ref[j, s, seg_len + POOL_PAD - POOL_HALO:seg_len + POOL_PAD, :]


def _mix_in(h, w_in_ref, cpad_ref, upad_ref, *, nseg, seg_len):
    z = jnp.dot(h, w_in_ref[...], preferred_element_type=jnp.float32)
    c = z[:, :D_CONV] * jax.nn.sigmoid(z[:, D_CONV:2 * D_CONV])
    u = z[:, 2 * D_CONV:]
    for s in range(nseg):
        for j in range(N_COLS):
            cpad_ref[j, s, CONV_PAD:CONV_PAD + seg_len, :] = c[s * seg_len:(s + 1) * seg_len, _col(j)]
            upad_ref[j, s, POOL_PAD:POOL_PAD + seg_len, :] = u[s * seg_len:(s + 1) * seg_len, _col(j)]


def _mix_rows(s, r0, conv_b_ref, ln_g_ref, ln_b_ref, cpad_ref, upad_ref, wtap_ref, mix_ref, dpool_ref,
              *, t, seg_len, pos0):
    f32, bf16 = jnp.float32, jnp.bfloat16
    m0 = s * seg_len + r0
    conv_b = conv_b_ref[...]
    reps = ROW_GROUP // SUBLANES
    cols = []
    for j in range(N_COLS):
        acc = jnp.broadcast_to(conv_b[:, _col(j)], (ROW_GROUP, LANES))
        for k in range(CONV_WIDTH):
            wk = wtap_ref[k, :, _col(j)]
            seg = cpad_ref[j, s, pl.ds(r0 + CONV_PAD - CONV_HALO + k, ROW_GROUP), :]
            acc = acc + seg * jnp.concatenate([wk] * reps, axis=0)
        cols.append(acc)
    cv = jnp.concatenate(cols, axis=-1)
    mu = jnp.mean(cv, axis=-1, keepdims=True)
    var = jnp.mean(jnp.square(cv - mu), axis=-1, keepdims=True)
    cv = (cv - mu) * lax.rsqrt(var + EPS) * ln_g_ref[...] + ln_b_ref[...]
    cv = cv * jax.nn.sigmoid(cv)
    mix_ref[m0:m0 + ROW_GROUP, 0:D_CONV] = cv.astype(bf16)

    for g, w in enumerate(POOL_WINDOWS):
        cur = upad_ref[g, s, pl.ds(r0 + POOL_PAD, ROW_GROUP), :]
        total = cur
        for i in range(1, w):
            total = total + upad_ref[g, s, pl.ds(r0 + POOL_PAD - i, ROW_GROUP), :]
        if r0 >= POOL_HALO:
            mean = total * (1.0 / w)
        else:
            pos = pos0 + t * seg_len + r0 + lax.broadcasted_iota(jnp.int32, (ROW_GROUP, LANES), 0)
            mean = total / jnp.minimum(w, pos + 1).astype(f32)
        dpool_ref[m0:m0 + ROW_GROUP, _col(g)] = (mean - cur).astype(bf16)


def _mix_out(x_ref, pool_w_ref, pool_scale_ref, w_out_ref, mix_ref, dpool_ref, *, rows):
    f32, bf16 = jnp.float32, jnp.bfloat16
    for g in range(len(POOL_WINDOWS)):
        pm = jnp.dot(dpool_ref[:, _col(g)], pool_w_ref[g], preferred_element_type=f32)
        pm = pm * pool_scale_ref[:, _col(g)]
        mix_ref[:, D_CONV + g * LANES:D_CONV + (g + 1) * LANES] = pm.astype(bf16)
    return (x_ref[...].reshape(rows, D_MODEL)
            + jnp.dot(mix_ref[...], w_out_ref[...], preferred_element_type=f32))


def _mlp_chunk(i, x, h, w_ff1_ref, w_ff2_ref):
    f0 = i * FF_CHUNK
    a = jnp.dot(h, w_ff1_ref[:, f0:f0 + FF_CHUNK], preferred_element_type=jnp.float32)
    a = jnp.square(jnp.maximum(a, 0.0)).astype(jnp.bfloat16)
    return x + jnp.dot(a, w_ff2_ref[f0:f0 + FF_CHUNK, :], preferred_element_type=jnp.float32)


def _ple_stage(x, p, g_ple_ref, w_ple_ref, w_gate_ref, g_final_ref, *, apply_final):
    f32, bf16 = jnp.float32, jnp.bfloat16
    pe = jnp.dot(p.astype(bf16), w_ple_ref[...], preferred_element_type=f32)
    h = _rms_scale(x, g_ple_ref[...]).astype(bf16)
    gate = jax.nn.sigmoid(jnp.dot(h, w_gate_ref[...], preferred_element_type=f32))
    x = x + gate * pe
    if apply_final:
        x = _rms_scale(x, g_final_ref[...])
    return x


def _layer_kernel(*refs, nseg, seg_len, tiles_per_seq, pos0, apply_final, lag, n_cast):
    bf16 = jnp.bfloat16
    it = iter(refs)
    take = lambda n: [next(it) for _ in range(n)]
    x_ref, p_ref, cinit_ref, uinit_ref = take(4)
    (conv_w_ref, conv_b_ref, ln_g_ref, ln_b_ref, pool_w_ref, pool_scale_ref, g_mix_ref, g_ffn_ref,
     g_ple_ref, w_ple_ref) = take(N_SMALL)
    w_in_ref, w_out_ref, w_ff1_ref, w_ff2_ref, w_gate_ref = take(N_BIG)
    g_final_ref, = take(1)
    cast_src_refs = take(n_cast)
    y_ref, newc_ref, newu_ref = take(3)
    cast_dst_refs = take(n_cast)
    cpad_ref, upad_ref, wtap_ref, mix_ref, dpool_ref = take(5)
    x1_ref, = take(lag) or [None]

    rows = nseg * seg_len
    n_chunks = D_FF // FF_CHUNK
    g = pl.program_id(0)
    ga = jnp.minimum(g, pl.num_programs(0) - 1 - lag)
    t = lax.rem(ga, tiles_per_seq)
    norm_in = lambda ref: _rms_scale(ref[...].reshape(rows, D_MODEL), g_mix_ref[...]).astype(bf16)

    @pl.when(g == 0)
    def _():
        for k in range(CONV_WIDTH):
            wtap_ref[k] = jnp.broadcast_to(conv_w_ref[pl.ds(k, 1), :], (SUBLANES, D_CONV))
        if lag:
            x1_ref[...] = jnp.zeros((rows, D_MODEL), jnp.float32)

    ple = functools.partial(_ple_stage, g_ple_ref=g_ple_ref, w_ple_ref=w_ple_ref, w_gate_ref=w_gate_ref,
                            g_final_ref=g_final_ref, apply_final=apply_final)
    row_groups = [
        functools.partial(_mix_rows, s, r0, conv_b_ref, ln_g_ref, ln_b_ref, cpad_ref, upad_ref, wtap_ref,
                          mix_ref, dpool_ref, t=t, seg_len=seg_len, pos0=pos0)
        for s in range(nseg) for r0 in range(0, seg_len, ROW_GROUP)]
    mix_in = functools.partial(_mix_in, w_in_ref=w_in_ref, cpad_ref=cpad_ref, upad_ref=upad_ref,
                               nseg=nseg, seg_len=seg_len)
    mix_out = functools.partial(_mix_out, x_ref, pool_w_ref, pool_scale_ref, w_out_ref, mix_ref,
                                dpool_ref, rows=rows)

    _load_history(cinit_ref, uinit_ref, cpad_ref, upad_ref, t=t, nseg=nseg, seg_len=seg_len)
    p = p_ref[...].reshape(rows, PLE_DIM)
    if lag:
        mix_in(norm_in(x_ref))
        x = x1_ref[...]
        h = _rms_scale(x, g_ffn_ref[...]).astype(bf16)
        for i in range(n_chunks):
            x = _mlp_chunk(i, x, h, w_ff1_ref, w_ff2_ref)
            for piece in row_groups[i::n_chunks]:
                piece()
        y_ref[...] = ple(x, p).reshape(nseg, seg_len, D_MODEL)
        x1_ref[...] = mix_out()
    else:
        mix_in(norm_in(x_ref))
        for piece in row_groups:
            piece()
        x = mix_out()
        h = _rms_scale(x, g_ffn_ref[...]).astype(bf16)
        for i in range(n_chunks):
            x = _mlp_chunk(i, x, h, w_ff1_ref, w_ff2_ref)
        y_ref[...] = ple(x, p).reshape(nseg, seg_len, D_MODEL)
    for src_ref, dst_ref in zip(cast_src_refs, cast_dst_refs):
        dst_ref[...] = src_ref[...].astype(bf16)
    _store_history(newc_ref, newu_ref, cpad_ref, upad_ref, t=t, last_t=tiles_per_seq - 1,
                   nseg=nseg, seg_len=seg_len)


def _layer_call(layer, x, p, cinit, uinit, small, big, g_final, cast_src, *, nseg, seg_len, pos0,
                apply_final, lag):
    batch, seq, _ = x.shape
    assert batch % nseg == 0 and seq % seg_len == 0
    assert seg_len % ROW_GROUP == 0 and seg_len >= CONV_PAD
    assert len(small) == N_SMALL and len(big) == N_BIG
    tiles_per_seq = seq // seg_len
    n_tiles = (batch // nseg) * tiles_per_seq
    rows = nseg * seg_len

    def tile(g):
        gc = jnp.clip(g, 0, n_tiles - 1)
        return gc // tiles_per_seq, gc % tiles_per_seq

    def stacked(w):
        zeros = (0,) * (w.ndim - 1)
        return pl.BlockSpec((None,) + w.shape[1:], lambda g: (layer,) + zeros,
                            pipeline_mode=pl.Buffered(1))

    def whole(w):
        zeros = (0,) * w.ndim
        return pl.BlockSpec(w.shape, lambda g: zeros, pipeline_mode=pl.Buffered(1))

    cast_rows = [w.shape[1] // n_tiles for w in cast_src]
    assert all(r % BF16_ROWS == 0 and r * n_tiles == w.shape[1] for r, w in zip(cast_rows, cast_src))

    tile_spec = lambda off: pl.BlockSpec((nseg, seg_len, D_MODEL), lambda g: tile(g + off) + (0,))
    in_specs = (
        [tile_spec(0)] + [
            pl.BlockSpec((None, nseg, seg_len, PLE_DIM), lambda g: (layer,) + tile(g - lag) + (0,)),
            pl.BlockSpec((None, nseg, CONV_HALO, D_CONV), lambda g: (layer, tile(g)[0], 0, 0)),
            pl.BlockSpec((None, nseg, POOL_HALO, D_POOL), lambda g: (layer, tile(g)[0], 0, 0)),
        ] + [stacked(w) for w in small] + [whole(w) for w in big] + [whole(g_final)] + [
            pl.BlockSpec((None, r, w.shape[2]), lambda g: (layer + 1, jnp.minimum(g, n_tiles - 1), 0))
            for r, w in zip(cast_rows, cast_src)])
    out_specs = [
        tile_spec(-lag),
        pl.BlockSpec((nseg, CONV_HALO, D_CONV), lambda g: (tile(g)[0], 0, 0)),
        pl.BlockSpec((nseg, POOL_HALO, D_POOL), lambda g: (tile(g)[0], 0, 0)),
    ] + [pl.BlockSpec((r, w.shape[2]), lambda g: (jnp.minimum(g, n_tiles - 1), 0))
         for r, w in zip(cast_rows, cast_src)]
    out_shape = [
        jax.ShapeDtypeStruct((batch, seq, D_MODEL), jnp.float32),
        jax.ShapeDtypeStruct((batch, CONV_HALO, D_CONV), jnp.float32),
        jax.ShapeDtypeStruct((batch, POOL_HALO, D_POOL), jnp.float32),
    ] + [jax.ShapeDtypeStruct(w.shape[1:], jnp.bfloat16) for w in cast_src]
    scratch_shapes = [
        pltpu.VMEM((N_COLS, nseg, CONV_PAD + seg_len, LANES), jnp.float32),
        pltpu.VMEM((N_COLS, nseg, POOL_PAD + seg_len, LANES), jnp.float32),
        pltpu.VMEM((CONV_WIDTH, SUBLANES, D_CONV), jnp.float32),
        pltpu.VMEM((rows, D_MODEL), jnp.bfloat16),
        pltpu.VMEM((rows, D_POOL), jnp.bfloat16),
    ] + [pltpu.VMEM((rows, D_MODEL), jnp.float32)] * lag
    operands = [x, p, cinit, uinit, *small, *big, g_final, *cast_src]
    outs = pl.pallas_call(
        functools.partial(_layer_kernel, nseg=nseg, seg_len=seg_len, tiles_per_seq=tiles_per_seq,
                          pos0=pos0, apply_final=apply_final, lag=lag, n_cast=len(cast_src)),
        grid=(n_tiles + lag,),
        in_specs=in_specs,
        out_specs=out_specs,
        out_shape=out_shape,
        scratch_shapes=scratch_shapes,
        compiler_params=pltpu.CompilerParams(
            dimension_semantics=("arbitrary",),
            vmem_limit_bytes=VMEM_LIMIT_BYTES),
        name=f"layer{layer}_{nseg}x{seg_len}",
    )(*operands)
    return outs[0], outs[1], outs[2], tuple(outs[3:])


def _tiling(batch, seq):
    seg_len = min(seq, TILE_ROWS)
    nseg = min(batch, TILE_ROWS // seg_len)
    lag = 1 if (batch // nseg) * (seq // seg_len) > 1 else 0
    return dict(nseg=nseg, seg_len=seg_len, lag=lag)


def kernel(x_prompt, x_sample, p_prompt, p_sample, cache_conv, cache_pool, w_in, conv_w, conv_b, ln_g, ln_b, pool_w, pool_scale, w_out, g_mix, g_ffn, g_ple, w_ff1, w_ff2, w_ple, w_gate, g_final):
    bf16 = jnp.bfloat16
    row = lambda v: v[:, None, :]
    depth, batch = p_prompt.shape[0], x_prompt.shape[0]
    small = (conv_w, row(conv_b), row(ln_g), row(ln_b), pool_w.astype(bf16), row(pool_scale),
             row(g_mix), row(g_ffn), row(g_ple), w_ple.astype(bf16))
    big_f32 = (w_in, w_out, w_ff1, w_ff2, w_gate)
    big = [tuple(w[0].astype(bf16) for w in big_f32)]
    g_fin = g_final[None, :]
    zero_conv = jnp.zeros((depth, batch, CONV_HALO, D_CONV), x_prompt.dtype)
    zero_pool = jnp.zeros((depth, batch, POOL_HALO, D_POOL), x_prompt.dtype)

    def trunk(x, p, cinit, uinit, pos0, cast):
        tiling = _tiling(x.shape[0], x.shape[1])
        new_c, new_u = [], []
        for layer in range(depth):
            cast_src = big_f32 if cast and layer + 1 < depth else ()
            x, nc, nu, rounded = _layer_call(layer, x, p, cinit, uinit, small, big[layer], g_fin, cast_src,
                                             pos0=pos0, apply_final=(layer == depth - 1), **tiling)
            if rounded:
                big.append(rounded)
            new_c.append(nc)
            new_u.append(nu)
        return x, jnp.stack(new_c, 0), jnp.stack(new_u, 0)

    y_p, nc_p, nu_p = trunk(x_prompt, p_prompt, zero_conv, zero_pool, 0, True)
    y_s, nc_s, nu_s = trunk(x_sample, p_sample, cache_conv, cache_pool, PAST_LEN, False)
    return (y_p, y_s, nc_p, nu_p, nc_s, nu_s)
```

```python
import functools

import jax
import jax.numpy as jnp
from jax import lax
from jax.experimental import pallas as pl
from jax.experimental.pallas import tpu as pltpu

D_MODEL = 1024
D_CONV = 512
D_POOL = 512
CONV_WIDTH = 31
CONV_HALO = CONV_WIDTH - 1
POOL_WINDOWS = (2, 4, 8, 16)
POOL_GROUP = D_POOL // len(POOL_WINDOWS)
POOL_HALO = max(POOL_WINDOWS) - 1
D_FF = 4 * D_MODEL
PLE_DIM = 256
PAST_LEN = 2048
EPS = 1e-6

SUBLANES = 8
LANES = 128
BF16_ROWS = 16
CONV_PAD = 32
POOL_PAD = 16
N_COLS = D_CONV // LANES
ROW_GROUP = 32
FF_CHUNK = 1024
TILE_ROWS = 512
VMEM_LIMIT_BYTES = 60000 * 1024
N_SMALL = 10
N_BIG = 5

assert POOL_GROUP == LANES and CONV_PAD >= CONV_HALO and POOL_PAD >= POOL_HALO


def _rms_scale(x, gain_row):
    ms = jnp.mean(x * x, axis=-1, keepdims=True)
    return x * lax.rsqrt(ms + EPS) * gain_row


def _col(j):
    return slice(j * LANES, (j + 1) * LANES)


def _load_history(cinit_ref, uinit_ref, cpad_ref, upad_ref, *, t, nseg, seg_len):
    @pl.when(t == 0)
    def _():
        for s in range(nseg):
            for j in range(N_COLS):
                cpad_ref[j, s, 0:SUBLANES, :] = jnp.zeros((SUBLANES, LANES), jnp.float32)
                cpad_ref[j, s, CONV_PAD - CONV_HALO:CONV_PAD, :] = cinit_ref[s, :, _col(j)]
                upad_ref[j, s, 0:SUBLANES, :] = jnp.zeros((SUBLANES, LANES), jnp.float32)
                upad_ref[j, s, POOL_PAD - POOL_HALO:POOL_PAD, :] = uinit_ref[s, :, _col(j)]

    @pl.when(t > 0)
    def _():
        for s in range(nseg):
            for j in range(N_COLS):
                cpad_ref[j, s, 0:CONV_PAD, :] = cpad_ref[j, s, seg_len:seg_len + CONV_PAD, :]
                upad_ref[j, s, 0:POOL_PAD, :] = upad_ref[j, s, seg_len:seg_len + POOL_PAD, :]


def _store_history(newc_ref, newu_ref, cpad_ref, upad_ref, *, t, last_t, nseg, seg_len):
    @pl.when(t == last_t)
    def _():
        for s in range(nseg):
            for j in range(N_COLS):
                newc_ref[s, :, _col(j)] = cpad_ref[j, s, seg_len + CONV_PAD - CONV_HALO:seg_len + CONV_PAD, :]
                newu_ref[s, :, _col(j)] = upad_ref[j, s, seg_len + POOL_PAD - POOL_HALO:seg_len + POOL_PAD, :]


def _mix_in(h, w_in_ref, cpad_ref, upad_ref, *, nseg, seg_len):
    z = jnp.dot(h, w_in_ref[...], preferred_element_type=jnp.float32)
    c = z[:, :D_CONV] * jax.nn.sigmoid(z[:, D_CONV:2 * D_CONV])
    u = z[:, 2 * D_CONV:]
    for s in range(nseg):
        for j in range(N_COLS):
            cpad_ref[j, s, CONV_PAD:CONV_PAD + seg_len, :] = c[s * seg_len:(s + 1) * seg_len, _col(j)]
            upad_ref[j, s, POOL_PAD:POOL_PAD + seg_len, :] = u[s * seg_len:(s + 1) * seg_len, _col(j)]


def _mix_rows(s, r0, conv_b_ref, ln_g_ref, ln_b_ref, cpad_ref, upad_ref, wtap_ref, mix_ref, dpool_ref,
              *, t, seg_len, pos0):
    f32, bf16 = jnp.float32, jnp.bfloat16
    m0 = s * seg_len + r0
    conv_b = conv_b_ref[...]
    reps = ROW_GROUP // SUBLANES
    cols = []
    for j in range(N_COLS):
        acc = jnp.broadcast_to(conv_b[:, _col(j)], (ROW_GROUP, LANES))
        for k in range(CONV_WIDTH):
            wk = wtap_ref[k, :, _col(j)]
            seg = cpad_ref[j, s, pl.ds(r0 + CONV_PAD - CONV_HALO + k, ROW_GROUP), :]
            acc = acc + seg * jnp.concatenate([wk] * reps, axis=0)
        cols.append(acc)
    cv = jnp.concatenate(cols, axis=-1)
    mu = jnp.mean(cv, axis=-1, keepdims=True)
    var = jnp.mean(jnp.square(cv - mu), axis=-1, keepdims=True)
    cv = (cv - mu) * lax.rsqrt(var + EPS) * ln_g_ref[...] + ln_b_ref[...]
    cv = cv * jax.nn.sigmoid(cv)
    mix_ref[m0:m0 + ROW_GROUP, 0:D_CONV] = cv.astype(bf16)

    for g, w in enumerate(POOL_WINDOWS):
        cur = upad_ref[g, s, pl.ds(r0 + POOL_PAD, ROW_GROUP), :]
        total = cur
        for i in range(1, w):
            total = total + upad_ref[g, s, pl.ds(r0 + POOL_PAD - i, ROW_GROUP), :]
        if r0 >= POOL_HALO:
            mean = total * (1.0 / w)
        else:
            pos = pos0 + t * seg_len + r0 + lax.broadcasted_iota(jnp.int32, (ROW_GROUP, LANES), 0)
            mean = total / jnp.minimum(w, pos + 1).astype(f32)
        dpool_ref[m0:m0 + ROW_GROUP, _col(g)] = (mean - cur).astype(bf16)


def _mix_out(x_ref, pool_w_ref, pool_scale_ref, w_out_ref, mix_ref, dpool_ref, *, rows):
    f32, bf16 = jnp.float32, jnp.bfloat16
    for g in range(len(POOL_WINDOWS)):
        pm = jnp.dot(dpool_ref[:, _col(g)], pool_w_ref[g], preferred_element_type=f32)
        pm = pm * pool_scale_ref[:, _col(g)]
        mix_ref[:, D_CONV + g * LANES:D_CONV + (g + 1) * LANES] = pm.astype(bf16)
    return (x_ref[...].reshape(rows, D_MODEL)
            + jnp.dot(mix_ref[...], w_out_ref[...], preferred_element_type=f32))


def _mlp_chunk(i, x, h, w_ff1_ref, w_ff2_ref):
    f0 = i * FF_CHUNK
    a = jnp.dot(h, w_ff1_ref[:, f0:f0 + FF_CHUNK], preferred_element_type=jnp.float32)
    a = jnp.square(jnp.maximum(a, 0.0)).astype(jnp.bfloat16)
    return x + jnp.dot(a, w_ff2_ref[f0:f0 + FF_CHUNK, :], preferred_element_type=jnp.float32)


def _ple_stage(x, pe, g_ple_ref, w_gate_ref, g_final_ref, *, apply_final):
    f32, bf16 = jnp.float32, jnp.bfloat16
    h = _rms_scale(x, g_ple_ref[...]).astype(bf16)
    gate = jax.nn.sigmoid(jnp.dot(h, w_gate_ref[...], preferred_element_type=f32))
    x = x + gate * pe
    if apply_final:
        x = _rms_scale(x, g_final_ref[...])
    return x


def _layer_kernel(*refs, nseg, seg_len, tiles_per_seq, pos0, apply_final, lag, n_cast):
    bf16 = jnp.bfloat16
    it = iter(refs)
    take = lambda n: [next(it) for _ in range(n)]
    x_ref, p_ref, cinit_ref, uinit_ref = take(4)
    (conv_w_ref, conv_b_ref, ln_g_ref, ln_b_ref, pool_w_ref, pool_scale_ref, g_mix_ref, g_ffn_ref,
     g_ple_ref, w_ple_ref) = take(N_SMALL)
    w_in_ref, w_out_ref, w_ff1_ref, w_ff2_ref, w_gate_ref = take(N_BIG)
    g_final_ref, = take(1)
    cast_src_refs = take(n_cast)
    y_ref, newc_ref, newu_ref = take(3)
    cast_dst_refs = take(n_cast)
    cpad_ref, upad_ref, wtap_ref, mix_ref, dpool_ref = take(5)
    x1_ref, = take(lag) or [None]

    rows = nseg * seg_len
    n_chunks = D_FF // FF_CHUNK
    g = pl.program_id(0)
    ga = jnp.minimum(g, pl.num_programs(0) - 1 - lag)
    t = lax.rem(ga, tiles_per_seq)
    norm_in = lambda ref: _rms_scale(ref[...].reshape(rows, D_MODEL), g_mix_ref[...]).astype(bf16)

    @pl.when(g == 0)
    def _():
        for k in range(CONV_WIDTH):
            wtap_ref[k] = jnp.broadcast_to(conv_w_ref[pl.ds(k, 1), :], (SUBLANES, D_CONV))

    ple = functools.partial(_ple_stage, g_ple_ref=g_ple_ref, w_gate_ref=w_gate_ref,
                            g_final_ref=g_final_ref, apply_final=apply_final)
    row_groups = [
        functools.partial(_mix_rows, s, r0, conv_b_ref, ln_g_ref, ln_b_ref, cpad_ref, upad_ref, wtap_ref,
                          mix_ref, dpool_ref, t=t, seg_len=seg_len, pos0=pos0)
        for s in range(nseg) for r0 in range(0, seg_len, ROW_GROUP)]
    mix_in = functools.partial(_mix_in, w_in_ref=w_in_ref, cpad_ref=cpad_ref, upad_ref=upad_ref,
                               nseg=nseg, seg_len=seg_len)
    mix_out = functools.partial(_mix_out, x_ref, pool_w_ref, pool_scale_ref, w_out_ref, mix_ref,
                                dpool_ref, rows=rows)

    def embed():
        return jnp.dot(p_ref[...].reshape(rows, PLE_DIM).astype(bf16), w_ple_ref[...],
                       preferred_element_type=jnp.float32)

    def mix_stage():
        mix_in(norm_in(x_ref))
        for piece in row_groups:
            piece()
        return mix_out()

    def mlp_stage(x, pe, between=()):
        h = _rms_scale(x, g_ffn_ref[...]).astype(bf16)
        for i in range(n_chunks):
            x = _mlp_chunk(i, x, h, w_ff1_ref, w_ff2_ref)
            for piece in between[i::n_chunks]:
                piece()
        return ple(x, pe).reshape(nseg, seg_len, D_MODEL)

    def round_next_weights():
        for src_ref, dst_ref in zip(cast_src_refs, cast_dst_refs):
            dst_ref[...] = src_ref[...].astype(bf16)

    _load_history(cinit_ref, uinit_ref, cpad_ref, upad_ref, t=t, nseg=nseg, seg_len=seg_len)
    if lag:
        last = pl.num_programs(0) - 1

        @pl.when(g == 0)
        def _():
            x1_ref[...] = mix_stage()
            round_next_weights()

        @pl.when(jnp.logical_and(g > 0, g < last))
        def _():
            pe = embed()
            mix_in(norm_in(x_ref))
            y_ref[...] = mlp_stage(x1_ref[...], pe, between=row_groups)
            x1_ref[...] = mix_out()
            round_next_weights()

        @pl.when(g == last)
        def _():
            y_ref[...] = mlp_stage(x1_ref[...], embed())
    else:
        pe = embed()
        y_ref[...] = mlp_stage(mix_stage(), pe)
        round_next_weights()
    _store_history(newc_ref, newu_ref, cpad_ref, upad_ref, t=t, last_t=tiles_per_seq - 1,
                   nseg=nseg, seg_len=seg_len)


def _layer_call(layer, x, p, cinit, uinit, small, big, g_final, cast_src, *, nseg, seg_len, pos0,
                apply_final, lag):
    batch, seq, _ = x.shape
    assert batch % nseg == 0 and seq % seg_len == 0
    assert seg_len % ROW_GROUP == 0 and seg_len >= CONV_PAD
    assert len(small) == N_SMALL and len(big) == N_BIG
    tiles_per_seq = seq // seg_len
    n_tiles = (batch // nseg) * tiles_per_seq
    rows = nseg * seg_len

    def tile(g):
        gc = jnp.clip(g, 0, n_tiles - 1)
        return gc // tiles_per_seq, gc % tiles_per_seq

    def stacked(w):
        zeros = (0,) * (w.ndim - 1)
        return pl.BlockSpec((None,) + w.shape[1:], lambda g: (layer,) + zeros,
                            pipeline_mode=pl.Buffered(1))

    def whole(w):
        zeros = (0,) * w.ndim
        return pl.BlockSpec(w.shape, lambda g: zeros, pipeline_mode=pl.Buffered(1))

    cast_rows = [w.shape[1] // n_tiles for w in cast_src]
    assert all(r % BF16_ROWS == 0 and r * n_tiles == w.shape[1] for r, w in zip(cast_rows, cast_src))

    tile_spec = lambda off: pl.BlockSpec((nseg, seg_len, D_MODEL), lambda g: tile(g + off) + (0,))
    in_specs = (
        [tile_spec(0)] + [
            pl.BlockSpec((None, nseg, seg_len, PLE_DIM), lambda g: (layer,) + tile(g - lag) + (0,)),
            pl.BlockSpec((None, nseg, CONV_HALO, D_CONV), lambda g: (layer, tile(g)[0], 0, 0)),
            pl.BlockSpec((None, nseg, POOL_HALO, D_POOL), lambda g: (layer, tile(g)[0], 0, 0)),
        ] + [stacked(w) for w in small] + [whole(w) for w in big] + [whole(g_final)] + [
            pl.BlockSpec((None, r, w.shape[2]), lambda g: (layer + 1, jnp.minimum(g, n_tiles - 1), 0))
            for r, w in zip(cast_rows, cast_src)])
    out_specs = [
        tile_spec(-lag),
        pl.BlockSpec((nseg, CONV_HALO, D_CONV), lambda g: (tile(g)[0], 0, 0)),
        pl.BlockSpec((nseg, POOL_HALO, D_POOL), lambda g: (tile(g)[0], 0, 0)),
    ] + [pl.BlockSpec((r, w.shape[2]), lambda g: (jnp.minimum(g, n_tiles - 1), 0))
         for r, w in zip(cast_rows, cast_src)]
    out_shape = [
        jax.ShapeDtypeStruct((batch, seq, D_MODEL), jnp.float32),
        jax.ShapeDtypeStruct((batch, CONV_HALO, D_CONV), jnp.float32),
        jax.ShapeDtypeStruct((batch, POOL_HALO, D_POOL), jnp.float32),
    ] + [jax.ShapeDtypeStruct(w.shape[1:], jnp.bfloat16) for w in cast_src]
    scratch_shapes = [
        pltpu.VMEM((N_COLS, nseg, CONV_PAD + seg_len, LANES), jnp.float32),
        pltpu.VMEM((N_COLS, nseg, POOL_PAD + seg_len, LANES), jnp.float32),
        pltpu.VMEM((CONV_WIDTH, SUBLANES, D_CONV), jnp.float32),
        pltpu.VMEM((rows, D_MODEL), jnp.bfloat16),
        pltpu.VMEM((rows, D_POOL), jnp.bfloat16),
    ] + [pltpu.VMEM((rows, D_MODEL), jnp.float32)] * lag
    operands = [x, p, cinit, uinit, *small, *big, g_final, *cast_src]
    outs = pl.pallas_call(
        functools.partial(_layer_kernel, nseg=nseg, seg_len=seg_len, tiles_per_seq=tiles_per_seq,
                          pos0=pos0, apply_final=apply_final, lag=lag, n_cast=len(cast_src)),
        grid=(n_tiles + lag,),
        in_specs=in_specs,
        out_specs=out_specs,
        out_shape=out_shape,
        scratch_shapes=scratch_shapes,
        compiler_params=pltpu.CompilerParams(
            dimension_semantics=("arbitrary",),
            vmem_limit_bytes=VMEM_LIMIT_BYTES),
        name=f"layer{layer}_{nseg}x{seg_len}",
    )(*operands)
    return outs[0], outs[1], outs[2], tuple(outs[3:])


def _tiling(batch, seq):
    seg_len = min(seq, TILE_ROWS)
    nseg = min(batch, TILE_ROWS // seg_len)
    lag = 1 if (batch // nseg) * (seq // seg_len) > 1 else 0
    return dict(nseg=nseg, seg_len=seg_len, lag=lag)


def kernel(x_prompt, x_sample, p_prompt, p_sample, cache_conv, cache_pool, w_in, conv_w, conv_b, ln_g, ln_b, pool_w, pool_scale, w_out, g_mix, g_ffn, g_ple, w_ff1, w_ff2, w_ple, w_gate, g_final):
    bf16 = jnp.bfloat16
    row = lambda v: v[:, None, :]
    depth, batch = p_prompt.shape[0], x_prompt.shape[0]
    small = (conv_w, row(conv_b), row(ln_g), row(ln_b), pool_w.astype(bf16), row(pool_scale),
             row(g_mix), row(g_ffn), row(g_ple), w_ple.astype(bf16))
    big_f32 = (w_in, w_out, w_ff1, w_ff2, w_gate)
    big = [tuple(w[0].astype(bf16) for w in big_f32)]
    g_fin = g_final[None, :]
    zero_conv = jnp.zeros((depth, batch, CONV_HALO, D_CONV), x_prompt.dtype)
    zero_pool = jnp.zeros((depth, batch, POOL_HALO, D_POOL), x_prompt.dtype)

    def trunk(x, p, cinit, uinit, pos0, cast):
        tiling = _tiling(x.shape[0], x.shape[1])
        new_c, new_u = [], []
        for layer in range(depth):
            cast_src = big_f32 if cast and layer + 1 < depth else ()
            x, nc, nu, rounded = _layer_call(layer, x, p, cinit, uinit, small, big[layer], g_fin, cast_src,
                                             pos0=pos0, apply_final=(layer == depth - 1), **tiling)
            if rounded:
                big.append(rounded)
            new_c.append(nc)
            new_u.append(nu)
        return x, jnp.stack(new_c, 0), jnp.stack(new_u, 0)

    y_p, nc_p, nu_p = trunk(x_prompt, p_prompt, zero_conv, zero_pool, 0, True)
    y_s, nc_s, nu_s = trunk(x_sample, p_sample, cache_conv, cache_pool, PAST_LEN, False)
    return (y_p, y_s, nc_p, nu_p, nc_s, nu_s)
```
